```python
import jax
import jax.numpy as jnp
from jax import lax
import numpy as np


D_MODEL = 1024
BATCH = 32
SEQ = 2048
DEPTH = 4
DEC_BATCH = 8
DEC_SEQ = 4096
PAST_LEN = 128

GRID_W = 64
HEAD_DIM = 64
D_FOURIER = D_MODEL // 4
N_FOURIER_GROUPS = D_FOURIER // HEAD_DIM
D_ATTN = D_MODEL // 2
N_ATTN_HEADS = D_ATTN // HEAD_DIM
D_CONV = D_MODEL // 4
CONV_WIDTH = 31
D_MIX = D_FOURIER + D_ATTN + D_CONV
N_OUT_GROUPS = D_MIX // HEAD_DIM
D_IN_PROJ = D_FOURIER + 3 * D_ATTN + 2 * D_CONV
D_FF = 4 * D_MODEL
WIN_ROWS_MAX = 8
WIN_COLS = 16
QCOL_BLOCK = 16
KCOL_SPAN = QCOL_BLOCK + WIN_COLS
RMS_EPS = 1e-6
LN_EPS = 1e-5

kernel_name = 'hybrid_fnet_natten_conformer_encoder'


def rms_norm(x, g):
    xf = x.astype(jnp.float32)
    y = xf * lax.rsqrt(jnp.mean(xf * xf, axis=-1, keepdims=True) + RMS_EPS)
    return (y * g.astype(jnp.float32)).astype(x.dtype)


def layer_norm(x, g, b):
    xf = x.astype(jnp.float32)
    mu = jnp.mean(xf, axis=-1, keepdims=True)
    xc = xf - mu
    var = jnp.mean(xc * xc, axis=-1, keepdims=True)
    y = xc * lax.rsqrt(var + LN_EPS) * g.astype(jnp.float32) + b.astype(jnp.float32)
    return y.astype(x.dtype)


def fourier_mix(a):
    B, S, _ = a.shape
    ag = a.reshape(B, S, N_FOURIER_GROUPS, HEAD_DIM).astype(jnp.float32)
    f = jnp.fft.fft2(ag, axes=(1, 3), norm='ortho').real
    return f.reshape(B, S, D_FOURIER).astype(a.dtype)


def neighbourhood_attention(q, k, v, rpb):
    B, S, H, hd = q.shape
    rows = S // GRID_W
    kh = min(WIN_ROWS_MAX, rows)
    n_cb = GRID_W // QCOL_BLOCK
    q = q.reshape(B, rows, GRID_W, H, hd) * (hd ** -0.5)
    k = k.reshape(B, rows, GRID_W, H, hd)
    v = v.reshape(B, rows, GRID_W, H, hd)
    row_start = jnp.clip(jnp.arange(rows) - kh // 2, 0, rows - kh)
    cols = jnp.arange(GRID_W)
    col_start = jnp.clip(cols - WIN_COLS // 2, 0, GRID_W - WIN_COLS)
    span_start = jnp.clip(col_start[::QCOL_BLOCK], 0, GRID_W - KCOL_SPAN)
    span_cols = span_start[:, None] + jnp.arange(KCOL_SPAN)
    q_cols = cols.reshape(n_cb, QCOL_BLOCK)
    q_cs = col_start.reshape(n_cb, QCOL_BLOCK)
    kc = span_cols[:, None, :]
    col_valid = (kc >= q_cs[..., None]) & (kc < q_cs[..., None] + WIN_COLS)
    dc = jnp.clip(kc - q_cols[..., None] + WIN_COLS - 1, 0, 2 * WIN_COLS - 2)
    rpb_f = rpb.astype(jnp.float32)

    def row_block(r):
        rs = row_start[r]
        q_r = lax.dynamic_index_in_dim(q, r, axis=1, keepdims=False).reshape(B, n_cb, QCOL_BLOCK, H, hd)
        k_r = lax.dynamic_slice_in_dim(k, rs, kh, axis=1)[:, :, span_cols]
        v_r = lax.dynamic_slice_in_dim(v, rs, kh, axis=1)[:, :, span_cols]
        s = jnp.einsum('bnqhd,binshd->bhnqis', q_r, k_r).astype(jnp.float32)
        dr = rs + jnp.arange(kh) - r + WIN_ROWS_MAX - 1
        bias = rpb_f[:, dr[None, None, :, None], dc[:, :, None, :]]
        s = jnp.where(col_valid[:, :, None, :], s + bias, -jnp.inf)
        p = jax.nn.softmax(s.reshape(B, H, n_cb, QCOL_BLOCK, kh * KCOL_SPAN), axis=-1)
        p = p.reshape(s.shape).astype(v.dtype)
        o = jnp.einsum('bhnqis,binshd->bnqhd', p, v_r)
        return o.reshape(B, GRID_W, H, hd)

    out = lax.map(row_block, jnp.arange(rows))
    return jnp.moveaxis(out, 0, 1).reshape(B, S, H * hd)


def conformer_conv(u, conv_w, conv_b, ln_g, ln_b):
    g = u[..., :D_CONV] * jax.nn.sigmoid(u[..., D_CONV:])
    y = lax.conv_general_dilated(
        g, conv_w[:, None, :].astype(g.dtype), window_strides=(1,),
        padding=[(CONV_WIDTH // 2, CONV_WIDTH // 2)],
        dimension_numbers=('NWC', 'WIO', 'NWC'), feature_group_count=D_CONV)
    y = layer_norm(y + conv_b, ln_g, ln_b)
    return jax.nn.silu(y)


def encoder_layer(x, norm1_g, w_in, rpb, conv_w, conv_b, conv_ln_g, conv_ln_b,
                  mix_norm_g, w_out, norm2_g, w_ff_in, w_ff_out):
    B, S, _ = x.shape
    h = rms_norm(x, norm1_g)
    p = h @ w_in
    a = p[..., :D_FOURIER]
    o0 = D_FOURIER
    q = p[..., o0:o0 + D_ATTN].reshape(B, S, N_ATTN_HEADS, HEAD_DIM)
    k = p[..., o0 + D_ATTN:o0 + 2 * D_ATTN].reshape(B, S, N_ATTN_HEADS, HEAD_DIM)
    v = p[..., o0 + 2 * D_ATTN:o0 + 3 * D_ATTN].reshape(B, S, N_ATTN_HEADS, HEAD_DIM)
    u = p[..., o0 + 3 * D_ATTN:]
    o_f = fourier_mix(a)
    o_a = neighbourhood_attention(q, k, v, rpb)
    o_c = conformer_conv(u, conv_w, conv_b, conv_ln_g, conv_ln_b)
    o = jnp.concatenate([o_f, o_a, o_c], axis=-1).reshape(B, S, N_OUT_GROUPS, HEAD_DIM)
    o = rms_norm(o, mix_norm_g.reshape(N_OUT_GROUPS, HEAD_DIM)).reshape(B, S, D_MIX)
    x = x + o @ w_out
    h = rms_norm(x, norm2_g)
    f = jnp.square(jax.nn.relu(h @ w_ff_in))
    return x + f @ w_ff_out


def trunk(x, norm1_g, w_in, rpb, conv_w, conv_b, conv_ln_g, conv_ln_b,
          mix_norm_g, w_out, norm2_g, w_ff_in, w_ff_out, final_norm_g):
    for l in range(DEPTH):
        x = encoder_layer(x, norm1_g[l], w_in[l], rpb[l], conv_w[l], conv_b[l],
                          conv_ln_g[l], conv_ln_b[l], mix_norm_g[l], w_out[l],
                          norm2_g[l], w_ff_in[l], w_ff_out[l])
    return rms_norm(x, final_norm_g)


def setup_inputs(seed: int = 0) -> dict:
    key = jax.random.key(seed)
    ks = jax.random.split(key, 16)
    f32 = jnp.float32
    def nrm(k, shape, scale):
        return jax.random.normal(k, shape, f32) * scale
    return {
        'x_prompt': nrm(ks[0], (BATCH, SEQ, D_MODEL), 1.0),
        'x_sample': nrm(ks[1], (DEC_BATCH, DEC_SEQ, D_MODEL), 1.0),
        'norm1_g': 1.0 + nrm(ks[2], (DEPTH, D_MODEL), 0.01),
        'w_in': nrm(ks[3], (DEPTH, D_MODEL, D_IN_PROJ), D_MODEL ** -0.5),
        'rpb': nrm(ks[4], (DEPTH, N_ATTN_HEADS, 2 * WIN_ROWS_MAX - 1, 2 * WIN_COLS - 1), 0.02),
        'conv_w': nrm(ks[5], (DEPTH, CONV_WIDTH, D_CONV), CONV_WIDTH ** -0.5),
        'conv_b': nrm(ks[6], (DEPTH, D_CONV), 0.01),
        'conv_ln_g': 1.0 + nrm(ks[7], (DEPTH, D_CONV), 0.01),
        'conv_ln_b': nrm(ks[8], (DEPTH, D_CONV), 0.01),
        'mix_norm_g': 1.0 + nrm(ks[9], (DEPTH, D_MIX), 0.01),
        'w_out': nrm(ks[10], (DEPTH, D_MIX, D_MODEL), D_MIX ** -0.5),
        'norm2_g': 1.0 + nrm(ks[11], (DEPTH, D_MODEL), 0.01),
        'w_ff_in': nrm(ks[12], (DEPTH, D_MODEL, D_FF), D_MODEL ** -0.5),
        'w_ff_out': nrm(ks[13], (DEPTH, D_FF, D_MODEL), D_FF ** -0.5),
        'final_norm_g': 1.0 + nrm(ks[14], (D_MODEL,), 0.01),
    }


def reference(x_prompt, x_sample, norm1_g, w_in, rpb, conv_w, conv_b, conv_ln_g, conv_ln_b,
              mix_norm_g, w_out, norm2_g, w_ff_in, w_ff_out, final_norm_g):
    y_prompt = trunk(x_prompt, norm1_g, w_in, rpb, conv_w, conv_b, conv_ln_g, conv_ln_b,
                     mix_norm_g, w_out, norm2_g, w_ff_in, w_ff_out, final_norm_g)
    y_sample = trunk(x_sample, norm1_g, w_in, rpb, conv_w, conv_b, conv_ln_g, conv_ln_b,
                     mix_norm_g, w_out, norm2_g, w_ff_in, w_ff_out, final_norm_g)
    return (y_prompt, y_sample)
```

```python
import functools

import numpy as np
import jax
import jax.numpy as jnp
from jax import lax
from jax.experimental import pallas as pl
from jax.experimental.pallas import tpu as pltpu

D_MODEL = 1024
DEPTH = 4
GRID_W = 64
HEAD_DIM = 64
D_FOURIER = 256
N_FOURIER_GROUPS = 4
D_ATTN = 512
N_ATTN_HEADS = 8
D_CONV = 256
CONV_WIDTH = 31
CONV_HALF = CONV_WIDTH // 2
D_MIX = 1024
N_OUT_GROUPS = 16
D_IN_PROJ = 2304
D_FF = 4096
WIN_ROWS = 8
WIN_COLS = 16
RMS_EPS = 1e-6
LN_EPS = 1e-5
MASK_VALUE = -1e30

LANES = 128
HEADS_PER_BLOCK = LANES // HEAD_DIM
N_HEAD_BLOCKS = N_ATTN_HEADS // HEADS_PER_BLOCK
N_DR = 2 * WIN_ROWS - 1
N_DC = 2 * WIN_COLS - 1
KEYS_PER_ROW = WIN_ROWS * GRID_W
VMEM_LIMIT = 56 * 1024 * 1024

TOKEN_TILE = 512
CONV_TILE = 512
CONV_SUB = 32
CONV_PAD = 16

F32 = jnp.float32
BF16 = jnp.bfloat16


def _params(*sem):
    return pltpu.CompilerParams(dimension_semantics=sem, vmem_limit_bytes=VMEM_LIMIT)


def _const_spec(shape):
    return pl.BlockSpec(shape, lambda *_: (0,) * len(shape), pipeline_mode=pl.Buffered(1))


def _inproj_kernel(x_ref, g_ref, w_ref, cdft_ref, y_ref, q_ref, k_ref, v_ref, u_ref):
    x = x_ref[0]
    ms = jnp.mean(x * x, axis=-1, keepdims=True)
    h = (x * lax.rsqrt(ms + RMS_EPS) * g_ref[...]).astype(BF16)
    p = jnp.dot(h, w_ref[...], preferred_element_type=F32)
    a = p[:, :D_FOURIER].astype(BF16)
    y = jnp.dot(a, cdft_ref[...], preferred_element_type=F32)
    y_ref[0] = y[:, :D_FOURIER].astype(BF16)
    y_ref[1] = y[:, D_FOURIER:].astype(BF16)
    o0 = D_FOURIER
    q_ref[0] = (p[:, o0:o0 + D_ATTN] * (HEAD_DIM ** -0.5)).astype(BF16)
    k_ref[0] = p[:, o0 + D_ATTN:o0 + 2 * D_ATTN].astype(BF16)
    v_ref[0] = p[:, o0 + 2 * D_ATTN:o0 + 3 * D_ATTN].astype(BF16)
    o1 = o0 + 3 * D_ATTN
    gate = 1.0 / (1.0 + jnp.exp(-p[:, o1 + D_CONV:]))
    u_ref[0] = (p[:, o1:o1 + D_CONV] * gate).astype(BF16)


def _inproj(x, g, w, cdft):
    B, S, _ = x.shape
    tm = TOKEN_TILE
    out_shape = (
        jax.ShapeDtypeStruct((2, S, B * D_FOURIER), BF16),
        jax.ShapeDtypeStruct((B, S, D_ATTN), BF16),
        jax.ShapeDtypeStruct((B, S, D_ATTN), BF16),
        jax.ShapeDtypeStruct((B, S, D_ATTN), BF16),
        jax.ShapeDtypeStruct((B, S, D_CONV), BF16),
    )
    tok = lambda w_: pl.BlockSpec((1, tm, w_), lambda b, t: (b, t, 0))
    return pl.pallas_call(
        _inproj_kernel,
        grid=(B, S // tm),
        in_specs=[tok(D_MODEL), _const_spec((1, D_MODEL)), _const_spec((D_MODEL, D_IN_PROJ)),
                  _const_spec((D_FOURIER, 2 * D_FOURIER))],
        out_specs=(pl.BlockSpec((2, tm, D_FOURIER), lambda b, t: (0, t, b)),
                   tok(D_ATTN), tok(D_ATTN), tok(D_ATTN), tok(D_CONV)),
        out_shape=out_shape,
        compiler_params=_params("parallel", "parallel"),
        name="inproj",
    )(x, g, w, cdft)


def _fourier_kernel(d_ref, y_ref, o_ref):
    o_ref[...] = jnp.dot(d_ref[...], y_ref[...], preferred_element_type=F32).astype(o_ref.dtype)


def _fourier(dmat, y):
    S, K = dmat.shape
    N = y.shape[1]
    tm = 1024 if K <= 4096 else 512
    tn = 512
    return pl.pallas_call(
        _fourier_kernel,
        grid=(S // tm, N // tn),
        in_specs=[pl.BlockSpec((tm, K), lambda i, j: (i, 0)),
                  pl.BlockSpec((K, tn), lambda i, j: (0, j))],
        out_specs=pl.BlockSpec((tm, tn), lambda i, j: (i, j)),
        out_shape=jax.ShapeDtypeStruct((S, N), BF16),
        compiler_params=_params("parallel", "parallel"),
        name="fourier",
    )(dmat, y)


def _attn_kernel(q_ref, k_ref, v_ref, bias_ref, o_ref, *, rows):
    lane = lax.broadcasted_iota(jnp.int32, (GRID_W, LANES), 1)
    first = lane < HEAD_DIM

    def body(r, carry):
        rs = jnp.clip(r - WIN_ROWS // 2, 0, rows - WIN_ROWS)
        d0 = rs - r + (WIN_ROWS - 1)
        qoff = pl.multiple_of(r * GRID_W, GRID_W)
        koff = pl.multiple_of(rs * GRID_W, GRID_W)
        qp = q_ref[0, pl.ds(qoff, GRID_W), :]
        zero = jnp.zeros_like(qp)
        qs = jnp.concatenate([jnp.where(first, qp, zero), jnp.where(first, zero, qp)], axis=0)
        kw = k_ref[0, pl.ds(koff, KEYS_PER_ROW), :]
        s = lax.dot_general(qs, kw, (((1,), (1,)), ((), ())), preferred_element_type=F32)
        bias = jnp.concatenate(
            [jnp.concatenate([bias_ref[hh, d0 + 2 * j] for j in range(WIN_ROWS // 2)], axis=1)
             for hh in range(HEADS_PER_BLOCK)], axis=0)
        s = s + bias
        m = jnp.max(s, axis=-1, keepdims=True)
        e = jnp.exp(s - m)
        l = jnp.sum(e, axis=-1, keepdims=True)
        vw = v_ref[0, pl.ds(koff, KEYS_PER_ROW), :]
        res = jnp.dot(e.astype(BF16), vw, preferred_element_type=F32)
        res = res * (1.0 / l)
        o = jnp.where(first, res[:GRID_W], res[GRID_W:])
        o_ref[0, pl.ds(qoff, GRID_W), :] = o.astype(o_ref.dtype)
        return carry

    lax.fori_loop(0, rows, body, 0)


def _attention(q, k, v, bias):
    B, S, _ = q.shape
    rows = S // GRID_W
    seq = pl.BlockSpec((1, S, LANES), lambda p, b: (b, 0, p))
    return pl.pallas_call(
        functools.partial(_attn_kernel, rows=rows),
        grid=(N_HEAD_BLOCKS, B),
        in_specs=[seq, seq, seq,
                  pl.BlockSpec((HEADS_PER_BLOCK, N_DR - 1, GRID_W, LANES), lambda p, b: (p, 0, 0, 0))],
        out_specs=seq,
        out_shape=jax.ShapeDtypeStruct((B, S, D_ATTN), BF16),
        compiler_params=_params("parallel", "parallel"),
        name="attn",
    )(q, k, v, bias)


def _conv_kernel(up_ref, uc_ref, un_ref, w_ref, cb_ref, lg_ref, lb_ref, o_ref, ext_ref):
    t = pl.program_id(1)
    nt = pl.num_programs(1)
    ch = uc_ref.shape[1]
    prev = up_ref[0, ch - CONV_PAD:, :].astype(F32)
    nxt = un_ref[0, :CONV_PAD, :].astype(F32)
    ext_ref[:CONV_PAD, :] = jnp.where(t > 0, prev, 0.0)
    ext_ref[CONV_PAD:CONV_PAD + ch, :] = uc_ref[0].astype(F32)
    ext_ref[CONV_PAD + ch:, :] = jnp.where(t < nt - 1, nxt, 0.0)
    cb = cb_ref[...]
    lg = lg_ref[...]
    lb = lb_ref[...]
    for c0 in range(0, ch, CONV_SUB):
        acc = jnp.zeros((CONV_SUB, D_CONV), F32)
        for j in range(CONV_WIDTH):
            off = c0 + j + CONV_PAD - CONV_HALF
            acc = acc + w_ref[j:j + 1, :] * ext_ref[off:off + CONV_SUB, :]
        y = acc + cb
        mu = jnp.mean(y, axis=-1, keepdims=True)
        yc = y - mu
        var = jnp.mean(yc * yc, axis=-1, keepdims=True)
        z = yc * lax.rsqrt(var + LN_EPS) * lg + lb
        o_ref[0, c0:c0 + CONV_SUB, :] = (z * (1.0 / (1.0 + jnp.exp(-z)))).astype(o_ref.dtype)


def _conv(u, w, cb, lg, lb):
    B, S, _ = u.shape
    ch = CONV_TILE
    nt = S // ch
    blk = lambda f: pl.BlockSpec((1, ch, D_CONV), f)
    return pl.pallas_call(
        _conv_kernel,
        grid=(B, nt),
        in_specs=[blk(lambda b, t: (b, jnp.maximum(t - 1, 0), 0)),
                  blk(lambda b, t: (b, t, 0)),
                  blk(lambda b, t: (b, jnp.minimum(t + 1, nt - 1), 0)),
                  _const_spec((CONV_WIDTH, D_CONV)), _const_spec((1, D_CONV)),
                  _const_spec((1, D_CONV)), _const_spec((1, D_CONV))],
        out_specs=blk(lambda b, t: (b, t, 0)),
        out_shape=jax.ShapeDtypeStruct((B, S, D_CONV), BF16),
        scratch_shapes=[pltpu.VMEM((ch + 2 * CONV_PAD, D_CONV), F32)],
        compiler_params=_params("parallel", "parallel"),
        name="conv",
    )(u, u, u, w, cb, lg, lb)


def _split_dot(a, b):
    hi = a.astype(BF16)
    lo = (a - hi.astype(F32)).astype(BF16)
    return (jnp.dot(hi, b, preferred_element_type=F32) + jnp.dot(lo, b, preferred_element_type=F32))


def _outffn_kernel(x_ref, of_ref, oa_ref, oc_ref, mg_ref, wo_ref, g2_ref, w1_ref, w2_ref,
                   grp_ref, grpt_ref, o_ref):
    o = jnp.concatenate([of_ref[...].astype(F32), oa_ref[0].astype(F32), oc_ref[0].astype(F32)], axis=-1)
    ssq = _split_dot(o * o, grp_ref[...])
    rinv = lax.rsqrt(ssq * (1.0 / HEAD_DIM) + RMS_EPS)
    scale = _split_dot(rinv, grpt_ref[...])
    on = (o * scale * mg_ref[...]).astype(BF16)
    x1 = x_ref[0] + jnp.dot(on, wo_ref[...], preferred_element_type=F32)
    ms = jnp.mean(x1 * x1, axis=-1, keepdims=True)
    h = (x1 * lax.rsqrt(ms + RMS_EPS) * g2_ref[...]).astype(BF16)
    f = jnp.maximum(jnp.dot(h, w1_ref[...], preferred_element_type=F32), 0.0)
    f = (f * f).astype(BF16)
    o_ref[0] = x1 + jnp.dot(f, w2_ref[...], preferred_element_type=F32)


def _outffn(x, of, oa, oc, mg, wo, g2, w1, w2, grp, grpt):
    B, S, _ = x.shape
    tm = TOKEN_TILE
    tok = lambda w_: pl.BlockSpec((1, tm, w_), lambda b, t: (b, t, 0))
    return pl.pallas_call(
        _outffn_kernel,
        grid=(B, S // tm),
        in_specs=[tok(D_MODEL), pl.BlockSpec((tm, D_FOURIER), lambda b, t: (t, b)),
                  tok(D_ATTN), tok(D_CONV),
                  _const_spec((1, D_MIX)), _const_spec((D_MIX, D_MODEL)), _const_spec((1, D_MODEL)),
                  _const_spec((D_MODEL, D_FF)), _const_spec((D_FF, D_MODEL)),
                  _const_spec((D_MIX, LANES)), _const_spec((LANES, D_MIX))],
        out_specs=tok(D_MODEL),
        out_shape=jax.ShapeDtypeStruct((B, S, D_MODEL), F32),
        compiler_params=_params("parallel", "parallel"),
        name="outffn",
    )(x, of, oa, oc, mg, wo, g2, w1, w2, grp, grpt)


def _final_kernel(x_ref, g_ref, o_ref):
    x = x_ref[0]
    ms = jnp.mean(x * x, axis=-1, keepdims=True)
    o_ref[0] = x * lax.rsqrt(ms + RMS_EPS) * g_ref[...]


def _final_norm(x, g):
    B, S, _ = x.shape
    tm = TOKEN_TILE
    tok = pl.BlockSpec((1, tm, D_MODEL), lambda b, t: (b, t, 0))
    return pl.pallas_call(
        _final_kernel,
        grid=(B, S // tm),
        in_specs=[tok, _const_spec((1, D_MODEL))],
        out_specs=tok,
        out_shape=jax.ShapeDtypeStruct((B, S, D_MODEL), F32),
        compiler_params=_params("parallel", "parallel"),
        name="final_norm",
    )(x, g)


def _channel_dft():
    c = np.arange(HEAD_DIM)
    ang = 2.0 * np.pi * ((c[:, None] * c[None, :]) % HEAD_DIM) / HEAD_DIM
    m = np.zeros((D_FOURIER, 2 * D_FOURIER), np.float64)
    for g in range(N_FOURIER_GROUPS):
        sl = slice(g * HEAD_DIM, (g + 1) * HEAD_DIM)
        m[sl, g * HEAD_DIM:(g + 1) * HEAD_DIM] = np.cos(ang)
        m[sl, D_FOURIER + g * HEAD_DIM:D_FOURIER + (g + 1) * HEAD_DIM] = np.sin(ang)
    return jnp.asarray(m * HEAD_DIM ** -0.5, BF16)


def _sequence_dft(S):
    hi = S // GRID_W
    sp = np.arange(S)
    a1 = 2.0 * np.pi * ((GRID_W * np.arange(hi)[:, None] * sp[None, :]) % S) / S
    a0 = 2.0 * np.pi * ((np.arange(GRID_W)[:, None] * sp[None, :]) % S) / S
    c1, s1 = (jnp.asarray(f(a1)[:, None, :], F32) for f in (np.cos, np.sin))
    c0, s0 = (jnp.asarray(f(a0)[None, :, :], F32) for f in (np.cos, np.sin))
    scale = S ** -0.5
    cos = ((c1 * c0 - s1 * s0) * scale).reshape(S, S)
    sin = ((s1 * c0 + c1 * s0) * scale).reshape(S, S)
    return jnp.concatenate([cos, -sin], axis=1).astype(BF16)


def _group_matrices():
    grp = np.zeros((D_MIX, LANES), np.float32)
    grp[np.arange(D_MIX), np.arange(D_MIX) // HEAD_DIM] = 1.0
    return jnp.asarray(grp, BF16), jnp.asarray(grp.T, BF16)


def _bias_tables(rpb):
    c = np.arange(GRID_W)
    cs = np.clip(c - WIN_COLS // 2, 0, GRID_W - WIN_COLS)
    kc = np.arange(GRID_W)
    valid = (kc[None, :] >= cs[:, None]) & (kc[None, :] < cs[:, None] + WIN_COLS)
    dc = np.clip(kc[None, :] - c[:, None] + WIN_COLS - 1, 0, N_DC - 1)
    t = jnp.where(jnp.asarray(valid), rpb.astype(F32)[:, :, :, dc], MASK_VALUE)
    return jnp.concatenate([t[:, :, :-1], t[:, :, 1:]], axis=-1)


def _trunk(x, consts, params):
    B, S, _ = x.shape
    cdft, dmat, grp, grpt = consts
    for l in range(DEPTH):
        p = {k: v[l] for k, v in params.items()}
        y, q, k, v, u = _inproj(x, p["norm1_g"], p["w_in"], cdft)
        of = _fourier(dmat, y.reshape(2 * S, B * D_FOURIER))
        oa = _attention(q, k, v, p["bias"])
        oc = _conv(u, p["conv_w"], p["conv_b"], p["conv_ln_g"], p["conv_ln_b"])
        x = _outffn(x, of, oa, oc, p["mix_norm_g"], p["w_out"], p["norm2_g"], p["w_ff_in"],
                    p["w_ff_out"], grp, grpt)
    return x


def kernel(x_prompt, x_sample, norm1_g, w_in, rpb, conv_w, conv_b, conv_ln_g, conv_ln_b, mix_norm_g,
           w_out, norm2_g, w_ff_in, w_ff_out, final_norm_g):
    row = lambda a: a.astype(F32)[:, None, :]
    params = dict(
        norm1_g=row(norm1_g), w_in=w_in.astype(BF16), bias=_bias_tables(rpb),
        conv_w=conv_w.astype(F32), conv_b=row(conv_b), conv_ln_g=row(conv_ln_g), conv_ln_b=row(conv_ln_b),
        mix_norm_g=row(mix_norm_g), w_out=w_out.astype(BF16), norm2_g=row(norm2_g),
        w_ff_in=w_ff_in.astype(BF16), w_ff_out=w_ff_out.astype(BF16))
    cdft = _channel_dft()
    grp, grpt = _group_matrices()
    fg = final_norm_g.astype(F32)[None, :]
    outs = []
    for x in (x_prompt, x_sample):
        consts = (cdft, _sequence_dft(x.shape[1]), grp, grpt)
        outs.append(_final_norm(_trunk(x, consts, params), fg))
    return tuple(outs)
```

```python
import functools

import numpy as np
import jax
import jax.numpy as jnp
from jax import lax
from jax.experimental import pallas as pl
from jax.experimental.pallas import tpu as pltpu

D_MODEL = 1024
DEPTH = 4
GRID_W = 64
HEAD_DIM = 64
D_FOURIER = 256
N_FOURIER_GROUPS = 4
D_ATTN = 512
N_ATTN_HEADS = 8
D_CONV = 256
CONV_WIDTH = 31
CONV_HALF = CONV_WIDTH // 2
D_MIX = 1024
N_OUT_GROUPS = 16
D_IN_PROJ = 2304
D_FF = 4096
WIN_ROWS = 8
WIN_COLS = 16
RMS_EPS = 1e-6
LN_EPS = 1e-5
MASK_VALUE = -1e30

LANES = 128
SUBLANES = 8
HEADS_PER_BLOCK = LANES // HEAD_DIM
N_HEAD_BLOCKS = N_ATTN_HEADS // HEADS_PER_BLOCK
N_DR = 2 * WIN_ROWS - 1
N_DC = 2 * WIN_COLS - 1
KEYS_PER_ROW = WIN_ROWS * GRID_W
VMEM_LIMIT = 56 * 1024 * 1024

TOKEN_TILE = 512
CONV_TILE = 512
CONV_SUB = 32
CONV_PAD = 16
ATTN_ROWS_PER_STEP = 2
LOG2E = 1.4426950408889634

F32 = jnp.float32
BF16 = jnp.bfloat16


def _params(*sem):
    return pltpu.CompilerParams(dimension_semantics=sem, vmem_limit_bytes=VMEM_LIMIT)


def _const_spec(shape):
    return pl.BlockSpec(shape, lambda *_: (0,) * len(shape), pipeline_mode=pl.Buffered(1))


def _inproj_kernel(x_ref, g_ref, w_ref, cdft_ref, y_ref, q_ref, k_ref, v_ref, u_ref):
    x = x_ref[0]
    ms = jnp.mean(x * x, axis=-1, keepdims=True)
    h = (x * lax.rsqrt(ms + RMS_EPS) * g_ref[...]).astype(BF16)
    p = jnp.dot(h, w_ref[...], preferred_element_type=F32)
    a = p[:, :D_FOURIER].astype(BF16)
    y = jnp.dot(a, cdft_ref[...], preferred_element_type=F32)
    y_ref[0] = y[:, :D_FOURIER].astype(BF16)
    y_ref[1] = y[:, D_FOURIER:].astype(BF16)
    o0 = D_FOURIER
    q_ref[0] = (p[:, o0:o0 + D_ATTN] * (LOG2E * HEAD_DIM ** -0.5)).astype(BF16)
    k_ref[0] = p[:, o0 + D_ATTN:o0 + 2 * D_ATTN].astype(BF16)
    v_ref[0] = p[:, o0 + 2 * D_ATTN:o0 + 3 * D_ATTN].astype(BF16)
    o1 = o0 + 3 * D_ATTN
    gate = 1.0 / (1.0 + jnp.exp(-p[:, o1 + D_CONV:]))
    u_ref[0] = (p[:, o1:o1 + D_CONV] * gate).astype(BF16)


def _inproj(x, g, w, cdft):
    B, S, _ = x.shape
    tm = TOKEN_TILE
    out_shape = (
        jax.ShapeDtypeStruct((2, S, B * D_FOURIER), BF16),
        jax.ShapeDtypeStruct((B, S, D_ATTN), BF16),
        jax.ShapeDtypeStruct((B, S, D_ATTN), BF16),
        jax.ShapeDtypeStruct((B, S, D_ATTN), BF16),
        jax.ShapeDtypeStruct((B, S, D_CONV), BF16),
    )
    tok = lambda w_: pl.BlockSpec((1, tm, w_), lambda b, t: (b, t, 0))
    return pl.pallas_call(
        _inproj_kernel,
        grid=(B, S // tm),
        in_specs=[tok(D_MODEL), _const_spec((1, D_MODEL)), _const_spec((D_MODEL, D_IN_PROJ)),
                  _const_spec((D_FOURIER, 2 * D_FOURIER))],
        out_specs=(pl.BlockSpec((2, tm, D_FOURIER), lambda b, t: (0, t, b)),
                   tok(D_ATTN), tok(D_ATTN), tok(D_ATTN), tok(D_CONV)),
        out_shape=out_shape,
        compiler_params=_params("parallel", "parallel"),
        name="inproj",
    )(x, g, w, cdft)


def _fourier_kernel(d_ref, y_ref, o_ref):
    o_ref[...] = jnp.dot(d_ref[...], y_ref[...], preferred_element_type=F32).astype(o_ref.dtype)


def _fourier(dmat, y):
    S, K = dmat.shape
    N = y.shape[1]
    tm = 1024 if K <= 4096 else 512
    tn = 512
    return pl.pallas_call(
        _fourier_kernel,
        grid=(S // tm, N // tn),
        in_specs=[pl.BlockSpec((tm, K), lambda i, j: (i, 0)),
                  pl.BlockSpec((K, tn), lambda i, j: (0, j))],
        out_specs=pl.BlockSpec((tm, tn), lambda i, j: (i, j)),
        out_shape=jax.ShapeDtypeStruct((S, N), BF16),
        compiler_params=_params("parallel", "parallel"),
        name="fourier",
    )(dmat, y)


def _attn_kernel(q_ref, k_ref, v_ref, bias_ref, o_ref, s0_ref, s1_ref, *, rows):
    lane = lax.broadcasted_iota(jnp.int32, (GRID_W, LANES), 1)
    first = lane < HEAD_DIM

    def window(r):
        rs = jnp.clip(r - WIN_ROWS // 2, 0, rows - WIN_ROWS)
        return (rs - r + (WIN_ROWS - 1), pl.multiple_of(r * GRID_W, GRID_W),
                pl.multiple_of(rs * GRID_W, GRID_W))

    def scores(r, hb):
        _, qoff, koff = window(r)
        cols = slice(hb * LANES, (hb + 1) * LANES)
        qp = q_ref[0, pl.ds(qoff, GRID_W), cols]
        zero = jnp.zeros_like(qp)
        qs = jnp.concatenate([jnp.where(first, qp, zero), jnp.where(first, zero, qp)], axis=0)
        kw = k_ref[0, pl.ds(koff, KEYS_PER_ROW), cols]
        return lax.dot_general(qs, kw, (((1,), (1,)), ((), ())), preferred_element_type=F32)

    def finish(r, hb, s):
        d0, qoff, koff = window(r)
        cols = slice(hb * LANES, (hb + 1) * LANES)
        bias = jnp.concatenate(
            [jnp.concatenate([bias_ref[HEADS_PER_BLOCK * hb + hh, d0 + 2 * j]
                              for j in range(WIN_ROWS // 2)], axis=1)
             for hh in range(HEADS_PER_BLOCK)], axis=0)
        s = s + bias
        m = jnp.max(s, axis=-1, keepdims=True)
        e = jnp.exp2(s - m)
        l = jnp.sum(e, axis=-1, keepdims=True)
        vw = v_ref[0, pl.ds(koff, KEYS_PER_ROW), cols]
        res = jnp.dot(e.astype(BF16), vw, preferred_element_type=F32)
        res = res * (1.0 / l)
        o = jnp.where(first, res[:GRID_W], res[GRID_W:])
        o_ref[0, pl.ds(qoff, GRID_W), cols] = o.astype(o_ref.dtype)

    steps = rows // ATTN_ROWS_PER_STEP
    chains = [(rr, hb) for rr in range(ATTN_ROWS_PER_STEP) for hb in range(N_HEAD_BLOCKS)]

    def score_step(i, buf):
        for c, (rr, hb) in enumerate(chains):
            buf[c] = scores(i * ATTN_ROWS_PER_STEP + rr, hb)

    def finish_step(i, buf):
        for c, (rr, hb) in enumerate(chains):
            finish(i * ATTN_ROWS_PER_STEP + rr, hb, buf[c])

    score_step(0, s0_ref)

    def body(j, carry):
        i = 2 * j
        finish_step(i, s0_ref)
        score_step(i + 1, s1_ref)
        finish_step(i + 1, s1_ref)
        score_step(jnp.minimum(i + 2, steps - 1), s0_ref)
        return carry

    lax.fori_loop(0, steps // 2, body, 0)


def _attention(q, k, v, bias):
    B, S, _ = q.shape
    rows = S // GRID_W
    seq = pl.BlockSpec((1, S, D_ATTN), lambda b: (b, 0, 0))
    return pl.pallas_call(
        functools.partial(_attn_kernel, rows=rows),
        grid=(B,),
        in_specs=[seq, seq, seq, _const_spec((N_ATTN_HEADS, N_DR - 1, GRID_W, LANES))],
        out_specs=seq,
        out_shape=jax.ShapeDtypeStruct((B, S, D_ATTN), BF16),
        scratch_shapes=[pltpu.VMEM((ATTN_ROWS_PER_STEP * N_HEAD_BLOCKS, HEADS_PER_BLOCK * GRID_W,
                                    KEYS_PER_ROW), F32)] * 2,
        compiler_params=_params("parallel"),
        name="attn",
    )(q, k, v, bias)


def _conv_kernel(up_ref, uc_ref, un_ref, w_ref, cb_ref, lg_ref, lb_ref, o_ref, ext_ref, sh_ref):
    t = pl.program_id(1)
    nt = pl.num_programs(1)
    ch = uc_ref.shape[1]
    span = ch + 2 * CONV_PAD - SUBLANES
    prev = up_ref[0, ch - CONV_PAD:, :].astype(F32)
    nxt = un_ref[0, :CONV_PAD, :].astype(F32)
    ext_ref[:CONV_PAD, :] = jnp.where(t > 0, prev, 0.0)
    ext_ref[CONV_PAD:CONV_PAD + ch, :] = uc_ref[0].astype(F32)
    ext_ref[CONV_PAD + ch:, :] = jnp.where(t < nt - 1, nxt, 0.0)
    for r in range(1, SUBLANES):
        sh_ref[r - 1] = ext_ref[r:r + span, :]
    cb = cb_ref[...]
    lg = lg_ref[...]
    lb = lb_ref[...]
    for c0 in range(0, ch, CONV_SUB):
        acc = jnp.zeros((CONV_SUB // SUBLANES, SUBLANES, D_CONV), F32)
        for j in range(CONV_WIDTH):
            off = c0 + j + CONV_PAD - CONV_HALF
            r = off % SUBLANES
            base = off - r
            rows = (ext_ref[base:base + CONV_SUB, :] if r == 0
                    else sh_ref[r - 1, base:base + CONV_SUB, :])
            acc = acc + w_ref[j] * rows.reshape(CONV_SUB // SUBLANES, SUBLANES, D_CONV)
        y = acc.reshape(CONV_SUB, D_CONV) + cb
        mu = jnp.mean(y, axis=-1, keepdims=True)
        yc = y - mu
        var = jnp.mean(yc * yc, axis=-1, keepdims=True)
        z = yc * lax.rsqrt(var + LN_EPS) * lg + lb
        o_ref[0, c0:c0 + CONV_SUB, :] = (z * (1.0 / (1.0 + jnp.exp(-z)))).astype(o_ref.dtype)


def _conv(u, w, cb, lg, lb):
    B, S, _ = u.shape
    ch = CONV_TILE
    nt = S // ch
    blk = lambda f: pl.BlockSpec((1, ch, D_CONV), f)
    return pl.pallas_call(
        _conv_kernel,
        grid=(B, nt),
        in_specs=[blk(lambda b, t: (b, jnp.maximum(t - 1, 0), 0)),
                  blk(lambda b, t: (b, t, 0)),
                  blk(lambda b, t: (b, jnp.minimum(t + 1, nt - 1), 0)),
                  _const_spec((CONV_WIDTH, SUBLANES, D_CONV)), _const_spec((1, D_CONV)),
                  _const_spec((1, D_CONV)), _const_spec((1, D_CONV))],
        out_specs=blk(lambda b, t: (b, t, 0)),
        out_shape=jax.ShapeDtypeStruct((B, S, D_CONV), BF16),
        scratch_shapes=[pltpu.VMEM((ch + 2 * CONV_PAD, D_CONV), F32),
                        pltpu.VMEM((SUBLANES - 1, ch + 2 * CONV_PAD - SUBLANES, D_CONV), F32)],
        compiler_params=_params("parallel", "parallel"),
        name="conv",
    )(u, u, u, w, cb, lg, lb)


def _outffn_kernel(x_ref, of_ref, oa_ref, oc_ref, mg_ref, wo_ref, g2_ref, w1_ref, w2_ref,
                   grp_ref, grpt_ref, fg_ref, o_ref, *, final_norm):
    o = jnp.concatenate([of_ref[...].astype(F32), oa_ref[0].astype(F32), oc_ref[0].astype(F32)], axis=-1)
    ssq = jnp.dot((o * o).astype(BF16), grp_ref[...], preferred_element_type=F32)
    rinv = lax.rsqrt(ssq * (1.0 / HEAD_DIM) + RMS_EPS)
    scale = jnp.dot(rinv.astype(BF16), grpt_ref[...], preferred_element_type=F32)
    on = (o * scale * mg_ref[...]).astype(BF16)
    x1 = x_ref[0] + jnp.dot(on, wo_ref[...], preferred_element_type=F32)
    ms = jnp.mean(x1 * x1, axis=-1, keepdims=True)
    h = (x1 * lax.rsqrt(ms + RMS_EPS) * g2_ref[...]).astype(BF16)
    f = jnp.maximum(jnp.dot(h, w1_ref[...], preferred_element_type=F32), 0.0)
    f = (f * f).astype(BF16)
    x2 = x1 + jnp.dot(f, w2_ref[...], preferred_element_type=F32)
    if final_norm:
        ms2 = jnp.mean(x2 * x2, axis=-1, keepdims=True)
        x2 = x2 * lax.rsqrt(ms2 + RMS_EPS) * fg_ref[...]
    o_ref[0] = x2


def _outffn(x, of, oa, oc, mg, wo, g2, w1, w2, grp, grpt, fg, final_norm):
    B, S, _ = x.shape
    tm = TOKEN_TILE
    tok = lambda w_: pl.BlockSpec((1, tm, w_), lambda b, t: (b, t, 0))
    return pl.pallas_call(
        functools.partial(_outffn_kernel, final_norm=final_norm),
        grid=(B, S // tm),
        in_specs=[tok(D_MODEL), pl.BlockSpec((tm, D_FOURIER), lambda b, t: (t, b)),
                  tok(D_ATTN), tok(D_CONV),
                  _const_spec((1, D_MIX)), _const_spec((D_MIX, D_MODEL)), _const_spec((1, D_MODEL)),
                  _const_spec((D_MODEL, D_FF)), _const_spec((D_FF, D_MODEL)),
                  _const_spec((D_MIX, LANES)), _const_spec((LANES, D_MIX)), _const_spec((1, D_MODEL))],
        out_specs=tok(D_MODEL),
        out_shape=jax.ShapeDtypeStruct((B, S, D_MODEL), F32),
        compiler_params=_params("parallel", "parallel"),
        name="outffn",
    )(x, of, oa, oc, mg, wo, g2, w1, w2, grp, grpt, fg)


def _channel_dft():
    c = np.arange(HEAD_DIM)
    ang = 2.0 * np.pi * ((c[:, None] * c[None, :]) % HEAD_DIM) / HEAD_DIM
    m = np.zeros((D_FOURIER, 2 * D_FOURIER), np.float64)
    for g in range(N_FOURIER_GROUPS):
        sl = slice(g * HEAD_DIM, (g + 1) * HEAD_DIM)
        m[sl, g * HEAD_DIM:(g + 1) * HEAD_DIM] = np.cos(ang)
        m[sl, D_FOURIER + g * HEAD_DIM:D_FOURIER + (g + 1) * HEAD_DIM] = np.sin(ang)
    return jnp.asarray(m * HEAD_DIM ** -0.5, BF16)


def _sequence_dft(S):
    hi = S // GRID_W
    sp = np.arange(S)
    a1 = 2.0 * np.pi * ((GRID_W * np.arange(hi)[:, None] * sp[None, :]) % S) / S
    a0 = 2.0 * np.pi * ((np.arange(GRID_W)[:, None] * sp[None, :]) % S) / S
    c1, s1 = (jnp.asarray(f(a1)[:, None, :], F32) for f in (np.cos, np.sin))
    c0, s0 = (jnp.asarray(f(a0)[None, :, :], F32) for f in (np.cos, np.sin))
    scale = S ** -0.5
    cos = ((c1 * c0 - s1 * s0) * scale).reshape(S, S)
    sin = ((s1 * c0 + c1 * s0) * scale).reshape(S, S)
    return jnp.concatenate([cos, -sin], axis=1).astype(BF16)


def _group_matrices():
    grp = np.zeros((D_MIX, LANES), np.float32)
    grp[np.arange(D_MIX), np.arange(D_MIX) // HEAD_DIM] = 1.0
    return jnp.asarray(grp, BF16), jnp.asarray(grp.T, BF16)


def _bias_tables(rpb):
    c = np.arange(GRID_W)
    cs = np.clip(c - WIN_COLS // 2, 0, GRID_W - WIN_COLS)
    kc = np.arange(GRID_W)
    valid = (kc[None, :] >= cs[:, None]) & (kc[None, :] < cs[:, None] + WIN_COLS)
    dc = np.clip(kc[None, :] - c[:, None] + WIN_COLS - 1, 0, N_DC - 1)
    t = jnp.where(jnp.asarray(valid), rpb.astype(F32)[:, :, :, dc] * LOG2E, MASK_VALUE)
    return jnp.concatenate([t[:, :, :-1], t[:, :, 1:]], axis=-1)


def _trunk(x, consts, params):
    B, S, _ = x.shape
    cdft, dmat, grp, grpt, fg = consts
    for l in range(DEPTH):
        p = {k: v[l] for k, v in params.items()}
        y, q, k, v, u = _inproj(x, p["norm1_g"], p["w_in"], cdft)
        of = _fourier(dmat, y.reshape(2 * S, B * D_FOURIER))
        oa = _attention(q, k, v, p["bias"])
        oc = _conv(u, p["conv_w"], p["conv_b"], p["conv_ln_g"], p["conv_ln_b"])
        x = _outffn(x, of, oa, oc, p["mix_norm_g"], p["w_out"], p["norm2_g"], p["w_ff_in"],
                    p["w_ff_out"], grp, grpt, fg, final_norm=(l == DEPTH - 1))
    return x


def kernel(x_prompt, x_sample, norm1_g, w_in, rpb, conv_w, conv_b, conv_ln_g, conv_ln_b, mix_norm_g,
           w_out, norm2_g, w_ff_in, w_ff_out, final_norm_g):
    row = lambda a: a.astype(F32)[:, None, :]
    params = dict(
        norm1_g=row(norm1_g), w_in=w_in.astype(BF16), bias=_bias_tables(rpb),
        conv_w=jnp.broadcast_to(conv_w.astype(F32)[:, :, None, :], (DEPTH, CONV_WIDTH, SUBLANES, D_CONV)),
        conv_b=row(conv_b), conv_ln_g=row(conv_ln_g), conv_ln_b=row(conv_ln_b),
        mix_norm_g=row(mix_norm_g), w_out=w_out.astype(BF16), norm2_g=row(norm2_g),
        w_ff_in=w_ff_in.astype(BF16), w_ff_out=w_ff_out.astype(BF16))
    cdft = _channel_dft()
    grp, grpt = _group_matrices()
    fg = final_norm_g.astype(F32)[None, :]
    outs = []
    for x in (x_prompt, x_sample):
        consts = (cdft, _sequence_dft(x.shape[1]), grp, grpt, fg)
        outs.append(_trunk(x, consts, params))
    return tuple(outs)
```

```python
import functools

import numpy as np
import jax
import jax.numpy as jnp
from jax import lax
from jax.experimental import pallas as pl
from jax.experimental.pallas import tpu as pltpu

D_MODEL = 1024
DEPTH = 4
GRID_W = 64
HEAD_DIM = 64
D_FOURIER = 256
N_FOURIER_GROUPS = 4
D_ATTN = 512
N_ATTN_HEADS = 8
D_CONV = 256
CONV_WIDTH = 31
CONV_HALF = CONV_WIDTH // 2
D_MIX = 1024
N_OUT_GROUPS = 16
D_IN_PROJ = 2304
D_FF = 4096
WIN_ROWS = 8
WIN_COLS = 16
RMS_EPS = 1e-6
LN_EPS = 1e-5
MASK_VALUE = -1e30

LANES = 128
SUBLANES = 8
HEADS_PER_BLOCK = LANES // HEAD_DIM
N_HEAD_BLOCKS = N_ATTN_HEADS // HEADS_PER_BLOCK
N_DR = 2 * WIN_ROWS - 1
N_DC = 2 * WIN_COLS - 1
KEYS_PER_ROW = WIN_ROWS * GRID_W
VMEM_LIMIT = 56 * 1024 * 1024

TOKEN_TILE = 512
CONV_TILE = 512
CONV_SUB = 32
CONV_PAD = 16
ATTN_ROWS_PER_STEP = 2
OUTFFN_ROW_GROUPS = 2
INPROJ_ROW_GROUPS = 2
LOG2E = 1.4426950408889634

F32 = jnp.float32
BF16 = jnp.bfloat16


def _params(*sem):
    return pltpu.CompilerParams(dimension_semantics=sem, vmem_limit_bytes=VMEM_LIMIT)


def _const_spec(shape):
    return pl.BlockSpec(shape, lambda *_: (0,) * len(shape), pipeline_mode=pl.Buffered(1))


def _inproj_kernel(x_ref, g_ref, w_ref, cdft_ref, y_ref, q_ref, k_ref, v_ref, u_ref):
    tm = x_ref.shape[1]
    for r0 in range(0, tm, tm // INPROJ_ROW_GROUPS):
        rs = slice(r0, r0 + tm // INPROJ_ROW_GROUPS)
        x = x_ref[0, rs, :]
        ms = jnp.mean(x * x, axis=-1, keepdims=True)
        h = (x * lax.rsqrt(ms + RMS_EPS) * g_ref[...]).astype(BF16)
        p = jnp.dot(h, w_ref[...], preferred_element_type=F32)
        a = p[:, :D_FOURIER].astype(BF16)
        y = jnp.dot(a, cdft_ref[...], preferred_element_type=F32)
        y_ref[0, rs, :] = y[:, :D_FOURIER].astype(BF16)
        y_ref[1, rs, :] = y[:, D_FOURIER:].astype(BF16)
        o0 = D_FOURIER
        q_ref[0, rs, :] = (p[:, o0:o0 + D_ATTN] * (LOG2E * HEAD_DIM ** -0.5)).astype(BF16)
        k_ref[0, rs, :] = p[:, o0 + D_ATTN:o0 + 2 * D_ATTN].astype(BF16)
        v_ref[0, rs, :] = p[:, o0 + 2 * D_ATTN:o0 + 3 * D_ATTN].astype(BF16)
        o1 = o0 + 3 * D_ATTN
        gate = 1.0 / (1.0 + jnp.exp(-p[:, o1 + D_CONV:]))
        u_ref[0, rs, :] = (p[:, o1:o1 + D_CONV] * gate).astype(BF16)


def _inproj(x, g, w, cdft):
    B, S, _ = x.shape
    tm = TOKEN_TILE
    out_shape = (
        jax.ShapeDtypeStruct((2, S, B * D_FOURIER), BF16),
        jax.ShapeDtypeStruct((B, S, D_ATTN), BF16),
        jax.ShapeDtypeStruct((B, S, D_ATTN), BF16),
        jax.ShapeDtypeStruct((B, S, D_ATTN), BF16),
        jax.ShapeDtypeStruct((B, S, D_CONV), BF16),
    )
    tok = lambda w_: pl.BlockSpec((1, tm, w_), lambda b, t: (b, t, 0))
    return pl.pallas_call(
        _inproj_kernel,
        grid=(B, S // tm),
        in_specs=[tok(D_MODEL), _const_spec((1, D_MODEL)), _const_spec((D_MODEL, D_IN_PROJ)),
                  _const_spec((D_FOURIER, 2 * D_FOURIER))],
        out_specs=(pl.BlockSpec((2, tm, D_FOURIER), lambda b, t: (0, t, b)),
                   tok(D_ATTN), tok(D_ATTN), tok(D_ATTN), tok(D_CONV)),
        out_shape=out_shape,
        compiler_params=_params("parallel", "parallel"),
        name="inproj",
    )(x, g, w, cdft)


def _fourier_kernel(d_ref, y_ref, o_ref):
    o_ref[...] = jnp.dot(d_ref[...], y_ref[...], preferred_element_type=F32).astype(o_ref.dtype)


def _fourier(dmat, y):
    S, K = dmat.shape
    N = y.shape[1]
    tm = 1024 if K <= 4096 else 512
    tn = 512
    return pl.pallas_call(
        _fourier_kernel,
        grid=(S // tm, N // tn),
        in_specs=[pl.BlockSpec((tm, K), lambda i, j: (i, 0)),
                  pl.BlockSpec((K, tn), lambda i, j: (0, j))],
        out_specs=pl.BlockSpec((tm, tn), lambda i, j: (i, j)),
        out_shape=jax.ShapeDtypeStruct((S, N), BF16),
        compiler_params=_params("parallel", "parallel"),
        name="fourier",
    )(dmat, y)


def _attn_kernel(q_ref, k_ref, v_ref, bias_ref, o_ref, s0_ref, s1_ref, *, rows):
    lane = lax.broadcasted_iota(jnp.int32, (GRID_W, LANES), 1)
    first = lane < HEAD_DIM

    def window(r):
        rs = jnp.clip(r - WIN_ROWS // 2, 0, rows - WIN_ROWS)
        return (rs - r + (WIN_ROWS - 1), pl.multiple_of(r * GRID_W, GRID_W),
                pl.multiple_of(rs * GRID_W, GRID_W))

    def scores(r, hb):
        _, qoff, koff = window(r)
        cols = slice(hb * LANES, (hb + 1) * LANES)
        qp = q_ref[0, pl.ds(qoff, GRID_W), cols]
        zero = jnp.zeros_like(qp)
        qs = jnp.concatenate([jnp.where(first, qp, zero), jnp.where(first, zero, qp)], axis=0)
        kw = k_ref[0, pl.ds(koff, KEYS_PER_ROW), cols]
        return lax.dot_general(qs, kw, (((1,), (1,)), ((), ())), preferred_element_type=F32)

    def finish(r, hb, s):
        d0, qoff, koff = window(r)
        cols = slice(hb * LANES, (hb + 1) * LANES)
        bias = jnp.concatenate(
            [jnp.concatenate([bias_ref[HEADS_PER_BLOCK * hb + hh, d0 + 2 * j]
                              for j in range(WIN_ROWS // 2)], axis=1)
             for hh in range(HEADS_PER_BLOCK)], axis=0)
        s = s + bias
        m = jnp.max(s, axis=-1, keepdims=True)
        e = jnp.exp2(s - m)
        l = jnp.sum(e, axis=-1, keepdims=True)
        vw = v_ref[0, pl.ds(koff, KEYS_PER_ROW), cols]
        res = jnp.dot(e.astype(BF16), vw, preferred_element_type=F32)
        res = res * (1.0 / l)
        o = jnp.where(first, res[:GRID_W], res[GRID_W:])
        o_ref[0, pl.ds(qoff, GRID_W), cols] = o.astype(o_ref.dtype)

    steps = rows // ATTN_ROWS_PER_STEP
    chains = [(rr, hb) for rr in range(ATTN_ROWS_PER_STEP) for hb in range(N_HEAD_BLOCKS)]

    def score_step(i, buf):
        for c, (rr, hb) in enumerate(chains):
            buf[c] = scores(i * ATTN_ROWS_PER_STEP + rr, hb)

    def finish_step(i, buf):
        for c, (rr, hb) in enumerate(chains):
            finish(i * ATTN_ROWS_PER_STEP + rr, hb, buf[c])

    score_step(0, s0_ref)

    def body(j, carry):
        i = 2 * j
        finish_step(i, s0_ref)
        score_step(i + 1, s1_ref)
        finish_step(i + 1, s1_ref)
        score_step(jnp.minimum(i + 2, steps - 1), s0_ref)
        return carry

    lax.fori_loop(0, steps // 2, body, 0)


def _attention(q, k, v, bias):
    B, S, _ = q.shape
    rows = S // GRID_W
    seq = pl.BlockSpec((1, S, D_ATTN), lambda b: (b, 0, 0))
    return pl.pallas_call(
        functools.partial(_attn_kernel, rows=rows),
        grid=(B,),
        in_specs=[seq, seq, seq, _const_spec((N_ATTN_HEADS, N_DR - 1, GRID_W, LANES))],
        out_specs=seq,
        out_shape=jax.ShapeDtypeStruct((B, S, D_ATTN), BF16),
        scratch_shapes=[pltpu.VMEM((ATTN_ROWS_PER_STEP * N_HEAD_BLOCKS, HEADS_PER_BLOCK * GRID_W,
                                    KEYS_PER_ROW), F32)] * 2,
        compiler_params=_params("parallel"),
        name="attn",
    )(q, k, v, bias)


def _conv_kernel(up_ref, uc_ref, un_ref, w_ref, cb_ref, lg_ref, lb_ref, o_ref, ext_ref, sh_ref):
    t = pl.program_id(1)
    nt = pl.num_programs(1)
    ch = uc_ref.shape[1]
    span = ch + 2 * CONV_PAD - SUBLANES
    prev = up_ref[0, ch - CONV_PAD:, :].astype(F32)
    nxt = un_ref[0, :CONV_PAD, :].astype(F32)
    ext_ref[:CONV_PAD, :] = jnp.where(t > 0, prev, 0.0)
    ext_ref[CONV_PAD:CONV_PAD + ch, :] = uc_ref[0].astype(F32)
    ext_ref[CONV_PAD + ch:, :] = jnp.where(t < nt - 1, nxt, 0.0)
    for r in range(1, SUBLANES):
        sh_ref[r - 1] = ext_ref[r:r + span, :]
    cb = cb_ref[...]
    lg = lg_ref[...]
    lb = lb_ref[...]
    for c0 in range(0, ch, CONV_SUB):
        acc = jnp.zeros((CONV_SUB // SUBLANES, SUBLANES, D_CONV), F32)
        for j in range(CONV_WIDTH):
            off = c0 + j + CONV_PAD - CONV_HALF
            r = off % SUBLANES
            base = off - r
            rows = (ext_ref[base:base + CONV_SUB, :] if r == 0
                    else sh_ref[r - 1, base:base + CONV_SUB, :])
            acc = acc + w_ref[j] * rows.reshape(CONV_SUB // SUBLANES, SUBLANES, D_CONV)
        y = acc.reshape(CONV_SUB, D_CONV) + cb
        mu = jnp.mean(y, axis=-1, keepdims=True)
        yc = y - mu
        var = jnp.mean(yc * yc, axis=-1, keepdims=True)
        z = yc * lax.rsqrt(var + LN_EPS) * lg + lb
        o_ref[0, c0:c0 + CONV_SUB, :] = (z * (1.0 / (1.0 + jnp.exp(-z)))).astype(o_ref.dtype)


def _conv(u, w, cb, lg, lb):
    B, S, _ = u.shape
    ch = CONV_TILE
    nt = S // ch
    blk = lambda f: pl.BlockSpec((1, ch, D_CONV), f)
    return pl.pallas_call(
        _conv_kernel,
        grid=(B, nt),
        in_specs=[blk(lambda b, t: (b, jnp.maximum(t - 1, 0), 0)),
                  blk(lambda b, t: (b, t, 0)),
                  blk(lambda b, t: (b, jnp.minimum(t + 1, nt - 1), 0)),
                  _const_spec((CONV_WIDTH, SUBLANES, D_CONV)), _const_spec((1, D_CONV)),
                  _const_spec((1, D_CONV)), _const_spec((1, D_CONV))],
        out_specs=blk(lambda b, t: (b, t, 0)),
        out_shape=jax.ShapeDtypeStruct((B, S, D_CONV), BF16),
        scratch_shapes=[pltpu.VMEM((ch + 2 * CONV_PAD, D_CONV), F32),
                        pltpu.VMEM((SUBLANES - 1, ch + 2 * CONV_PAD - SUBLANES, D_CONV), F32)],
        compiler_params=_params("parallel", "parallel"),
        name="conv",
    )(u, u, u, w, cb, lg, lb)


def _group_rms_scale(o):
    rows = o.shape[0]
    first = lax.broadcasted_iota(jnp.int32, (rows, LANES), 1) < HEAD_DIM
    parts = []
    for j in range(o.shape[1] // LANES):
        sq = jnp.square(o[:, j * LANES:(j + 1) * LANES])
        s0 = jnp.sum(jnp.where(first, sq, 0.0), axis=-1, keepdims=True)
        s1 = jnp.sum(jnp.where(first, 0.0, sq), axis=-1, keepdims=True)
        r0 = lax.rsqrt(s0 * (1.0 / HEAD_DIM) + RMS_EPS)
        r1 = lax.rsqrt(s1 * (1.0 / HEAD_DIM) + RMS_EPS)
        parts.append(jnp.where(first, r0, r1))
    return jnp.concatenate(parts, axis=-1)


def _outffn_kernel(x_ref, of_ref, oa_ref, oc_ref, mg_ref, wo_ref, g2_ref, w1_ref, w2_ref, fg_ref, o_ref,
                   *, final_norm):
    tm = x_ref.shape[1]
    for r0 in range(0, tm, tm // OUTFFN_ROW_GROUPS):
        rs = slice(r0, r0 + tm // OUTFFN_ROW_GROUPS)
        o = jnp.concatenate([of_ref[rs, :].astype(F32), oa_ref[0, rs, :].astype(F32),
                             oc_ref[0, rs, :].astype(F32)], axis=-1)
        on = (o * _group_rms_scale(o) * mg_ref[...]).astype(BF16)
        x1 = x_ref[0, rs, :] + jnp.dot(on, wo_ref[...], preferred_element_type=F32)
        ms = jnp.mean(x1 * x1, axis=-1, keepdims=True)
        h = (x1 * lax.rsqrt(ms + RMS_EPS) * g2_ref[...]).astype(BF16)
        f = jnp.maximum(jnp.dot(h, w1_ref[...], preferred_element_type=F32), 0.0)
        f = (f * f).astype(BF16)
        x2 = x1 + jnp.dot(f, w2_ref[...], preferred_element_type=F32)
        if final_norm:
            ms2 = jnp.mean(x2 * x2, axis=-1, keepdims=True)
            x2 = x2 * lax.rsqrt(ms2 + RMS_EPS) * fg_ref[...]
        o_ref[0, rs, :] = x2


def _outffn(x, of, oa, oc, mg, wo, g2, w1, w2, fg, final_norm):
    B, S, _ = x.shape
    tm = TOKEN_TILE
    tok = lambda w_: pl.BlockSpec((1, tm, w_), lambda b, t: (b, t, 0))
    return pl.pallas_call(
        functools.partial(_outffn_kernel, final_norm=final_norm),
        grid=(B, S // tm),
        in_specs=[tok(D_MODEL), pl.BlockSpec((tm, D_FOURIER), lambda b, t: (t, b)),
                  tok(D_ATTN), tok(D_CONV),
                  _const_spec((1, D_MIX)), _const_spec((D_MIX, D_MODEL)), _const_spec((1, D_MODEL)),
                  _const_spec((D_MODEL, D_FF)), _const_spec((D_FF, D_MODEL)), _const_spec((1, D_MODEL))],
        out_specs=tok(D_MODEL),
        out_shape=jax.ShapeDtypeStruct((B, S, D_MODEL), F32),
        compiler_params=_params("parallel", "parallel"),
        name="outffn",
    )(x, of, oa, oc, mg, wo, g2, w1, w2, fg)


def _channel_dft():
    c = np.arange(HEAD_DIM)
    ang = 2.0 * np.pi * ((c[:, None] * c[None, :]) % HEAD_DIM) / HEAD_DIM
    m = np.zeros((D_FOURIER, 2 * D_FOURIER), np.float64)
    for g in range(N_FOURIER_GROUPS):
        sl = slice(g * HEAD_DIM, (g + 1) * HEAD_DIM)
        m[sl, g * HEAD_DIM:(g + 1) * HEAD_DIM] = np.cos(ang)
        m[sl, D_FOURIER + g * HEAD_DIM:D_FOURIER + (g + 1) * HEAD_DIM] = np.sin(ang)
    return jnp.asarray(m * HEAD_DIM ** -0.5, BF16)


def _sequence_dft(S):
    hi = S // GRID_W
    sp = np.arange(S)
    a1 = 2.0 * np.pi * ((GRID_W * np.arange(hi)[:, None] * sp[None, :]) % S) / S
    a0 = 2.0 * np.pi * ((np.arange(GRID_W)[:, None] * sp[None, :]) % S) / S
    c1, s1 = (jnp.asarray(f(a1)[:, None, :], F32) for f in (np.cos, np.sin))
    c0, s0 = (jnp.asarray(f(a0)[None, :, :], F32) for f in (np.cos, np.sin))
    scale = S ** -0.5
    cos = ((c1 * c0 - s1 * s0) * scale).reshape(S, S)
    sin = ((s1 * c0 + c1 * s0) * scale).reshape(S, S)
    return jnp.concatenate([cos, -sin], axis=1).astype(BF16)


def _bias_tables(rpb):
    c = np.arange(GRID_W)
    cs = np.clip(c - WIN_COLS // 2, 0, GRID_W - WIN_COLS)
    kc = np.arange(GRID_W)
    valid = (kc[None, :] >= cs[:, None]) & (kc[None, :] < cs[:, None] + WIN_COLS)
    dc = np.clip(kc[None, :] - c[:, None] + WIN_COLS - 1, 0, N_DC - 1)
    t = jnp.where(jnp.asarray(valid), rpb.astype(F32)[:, :, :, dc] * LOG2E, MASK_VALUE)
    return jnp.concatenate([t[:, :, :-1], t[:, :, 1:]], axis=-1)


def _trunk(x, consts, params):
    B, S, _ = x.shape
    cdft, dmat, fg = consts
    for l in range(DEPTH):
        p = {k: v[l] for k, v in params.items()}
        y, q, k, v, u = _inproj(x, p["norm1_g"], p["w_in"], cdft)
        of = _fourier(dmat, y.reshape(2 * S, B * D_FOURIER))
        oa = _attention(q, k, v, p["bias"])
        oc = _conv(u, p["conv_w"], p["conv_b"], p["conv_ln_g"], p["conv_ln_b"])
        x = _outffn(x, of, oa, oc, p["mix_norm_g"], p["w_out"], p["norm2_g"], p["w_ff_in"],
                    p["w_ff_out"], fg, final_norm=(l == DEPTH - 1))
    return x


def kernel(x_prompt, x_sample, norm1_g, w_in, rpb, conv_w, conv_b, conv_ln_g, conv_ln_b, mix_norm_g,
           w_out, norm2_g, w_ff_in, w_ff_out, final_norm_g):
    row = lambda a: a.astype(F32)[:, None, :]
    params = dict(
        norm1_g=row(norm1_g), w_in=w_in.astype(BF16), bias=_bias_tables(rpb),
        conv_w=jnp.broadcast_to(conv_w.astype(F32)[:, :, None, :], (DEPTH, CONV_WIDTH, SUBLANES, D_CONV)),
        conv_b=row(conv_b), conv_ln_g=row(conv_ln_g), conv_ln_b=row(conv_ln_b),
        mix_norm_g=row(mix_norm_g), w_out=w_out.astype(BF16), norm2_g=row(norm2_g),
        w_ff_in=w_ff_in.astype(BF16), w_ff_out=w_ff_out.astype(BF16))
    cdft = _channel_dft()
    fg = final_norm_g.astype(F32)[None, :]
    outs = []
    for x in (x_prompt, x_sample):
        consts = (cdft, _sequence_dft(x.shape[1]), fg)
        outs.append(_trunk(x, consts, params))
    return tuple(outs)
```
